```python
import jax, jax.numpy as jnp
from jax import lax
import numpy as np

D_MODEL = 4096
BATCH = 4
SEQ = 4096
DEPTH = 1

MIX_WIDTH = D_MODEL
MLSTM_WIDTH = MIX_WIDTH // 2
MOBA_WIDTH = MIX_WIDTH - MLSTM_WIDTH
MLSTM_HEADS = 4
MLSTM_DV = MLSTM_WIDTH // MLSTM_HEADS
MLSTM_DK = MLSTM_DV // 2
MLSTM_QK_WIDTH = MLSTM_HEADS * MLSTM_DK
MLSTM_CHUNK = 128
CONV_WIDTH = 4
GATE_SOFTCAP = 15.0
MOBA_HEAD_DIM = 128
MOBA_HEADS = MOBA_WIDTH // MOBA_HEAD_DIM
MOBA_BLOCK = 256
MOBA_TOPK = 3
MOBA_QCHUNK = 16
ROPE_THETA = 10000.0
PEER_HEADS = 8
PEER_NKEYS = 128
PEER_EXPERTS = PEER_NKEYS * PEER_NKEYS
PEER_TOPK = 16
PEER_QDIM = 256
PEER_TCHUNK = 128
EPS = 1e-6

COL_MQ = 0
COL_MK = COL_MQ + MLSTM_QK_WIDTH
COL_MV = COL_MK + MLSTM_QK_WIDTH
COL_MO = COL_MV + MLSTM_WIDTH
COL_MI = COL_MO + MLSTM_WIDTH
COL_MF = COL_MI + MLSTM_HEADS
COL_AQ = COL_MF + MLSTM_HEADS
COL_AK = COL_AQ + MOBA_WIDTH
COL_AV = COL_AK + MOBA_WIDTH
IN_COLS = COL_AV + MOBA_WIDTH

kernel_name = "hymba_mlstm_moba_peer_adaln"


def rmsnorm(x, w):
    xf = x.astype(jnp.float32)
    y = xf * lax.rsqrt(jnp.mean(xf * xf, axis=-1, keepdims=True) + EPS)
    return (y * w.astype(jnp.float32)).astype(x.dtype)


def softcap(t):
    return GATE_SOFTCAP * jnp.tanh(t / GATE_SOFTCAP)


def rotary(t, positions):
    half = t.shape[-1] // 2
    inv = ROPE_THETA ** (-jnp.arange(half, dtype=jnp.float32) / half)
    ang = positions.astype(jnp.float32)[:, None] * inv[None, :]
    cos, sin = jnp.cos(ang), jnp.sin(ang)
    tf = t.astype(jnp.float32)
    t1, t2 = tf[..., :half], tf[..., half:]
    return jnp.concatenate([t1 * cos - t2 * sin, t2 * cos + t1 * sin], axis=-1).astype(t.dtype)


def causal_depthwise_conv(u, w, b):
    C = u.shape[-1]
    out = lax.conv_general_dilated(u, w[:, None, :].astype(u.dtype), window_strides=(1,),
                                   padding=[(CONV_WIDTH - 1, 0)],
                                   dimension_numbers=("NWC", "WIO", "NWC"),
                                   feature_group_count=C)
    return out + b.astype(u.dtype)


def mlstm_chunkwise(q, k, v, i_pre, f_pre):
    B, H, S, dk = q.shape
    L = MLSTM_CHUNK
    nc = S // L
    qf = q.astype(jnp.float32) * (dk ** -0.5)
    kf, vf = k.astype(jnp.float32), v.astype(jnp.float32)
    log_i = i_pre.astype(jnp.float32)
    log_f = jax.nn.log_sigmoid(f_pre.astype(jnp.float32))

    def chunks(t):
        return jnp.moveaxis(t.reshape(B, H, nc, L, *t.shape[3:]), 2, 0)

    causal = jnp.tril(jnp.ones((L, L), dtype=bool))

    def step(carry, inp):
        C, n, m = carry
        qc, kc, vc, li, lf = inp
        b = jnp.cumsum(lf, axis=-1)
        g = b[..., -1]
        dmat = b[..., :, None] - b[..., None, :] + li[..., None, :]
        dmat = jnp.where(causal, dmat, -jnp.inf)
        inter = b + m[..., None]
        m_t = jnp.maximum(inter, jnp.max(dmat, axis=-1))
        s = jnp.einsum("bhtd,bhsd->bhts", qc, kc) * jnp.exp(dmat - m_t[..., None])
        a_inter = jnp.exp(inter - m_t)
        num = jnp.einsum("bhts,bhsv->bhtv", s, vc) + a_inter[..., None] * jnp.einsum("bhtd,bhdv->bhtv", qc, C)
        den = jnp.sum(s, axis=-1) + a_inter * jnp.einsum("bhtd,bhd->bht", qc, n)
        h = num / jnp.maximum(jnp.abs(den), jnp.exp(-m_t))[..., None]
        wl = g[..., None] - b + li
        m_new = jnp.maximum(g + m, jnp.max(wl, axis=-1))
        ws = jnp.exp(wl - m_new[..., None])
        a = jnp.exp(g + m - m_new)
        C = a[..., None, None] * C + jnp.einsum("bhs,bhsd,bhsv->bhdv", ws, kc, vc)
        n = a[..., None] * n + jnp.einsum("bhs,bhsd->bhd", ws, kc)
        return (C, n, m_new), h

    init = (jnp.zeros((B, H, dk, v.shape[-1]), jnp.float32),
            jnp.zeros((B, H, dk), jnp.float32),
            jnp.zeros((B, H), jnp.float32))
    _, h = lax.scan(step, init, (chunks(qf), chunks(kf), chunks(vf), chunks(log_i), chunks(log_f)))
    return jnp.moveaxis(h, 0, 2).reshape(B, H, S, v.shape[-1])


def moba_attention(q, k, v):
    B, H, S, dh = q.shape
    s_pad = -(-S // MOBA_BLOCK) * MOBA_BLOCK
    pad = s_pad - S
    q, k, v = (jnp.pad(t, ((0, 0), (0, 0), (0, pad), (0, 0))) for t in (q, k, v))
    nb = s_pad // MOBA_BLOCK
    k_blocks = k.reshape(B, H, nb, MOBA_BLOCK, dh)
    v_blocks = v.reshape(B, H, nb, MOBA_BLOCK, dh)
    k_mean = jnp.mean(k_blocks.astype(jnp.float32), axis=3)
    gate = jnp.einsum("bhsd,bhnd->bhsn", q.astype(jnp.float32), k_mean)
    q_block = jnp.arange(s_pad) // MOBA_BLOCK
    past = jnp.arange(nb)[None, :] < q_block[:, None]
    gate = jnp.where(past, gate, -jnp.inf)
    n_sel = min(MOBA_TOPK, nb)
    _, sel = lax.top_k(gate, n_sel)
    sel_valid = sel < q_block[:, None]
    scale = dh ** -0.5
    nq = s_pad // MOBA_QCHUNK

    def to_chunks(t):
        return jnp.moveaxis(t.reshape(B, H, nq, MOBA_QCHUNK, *t.shape[3:]), 2, 0)

    b_idx = jnp.arange(B)[:, None, None, None]
    h_idx = jnp.arange(H)[None, :, None, None]

    def one_chunk(args):
        ci, qc, selc, validc = args
        start = ci * MOBA_QCHUNK
        own = start // MOBA_BLOCK
        k_own = lax.dynamic_index_in_dim(k_blocks, own, axis=2, keepdims=False)
        v_own = lax.dynamic_index_in_dim(v_blocks, own, axis=2, keepdims=False)
        k_sel = k_blocks[b_idx, h_idx, selc]
        v_sel = v_blocks[b_idx, h_idx, selc]
        s_sel = jnp.einsum("bhqd,bhqnkd->bhqnk", qc, k_sel, preferred_element_type=jnp.float32) * scale
        s_sel = jnp.where(validc[..., None], s_sel, -jnp.inf)
        s_own = jnp.einsum("bhqd,bhkd->bhqk", qc, k_own, preferred_element_type=jnp.float32) * scale
        q_pos = start + jnp.arange(MOBA_QCHUNK)
        k_pos = own * MOBA_BLOCK + jnp.arange(MOBA_BLOCK)
        s_own = jnp.where(k_pos[None, :] <= q_pos[:, None], s_own, -jnp.inf)
        s_all = jnp.concatenate([s_sel.reshape(B, H, MOBA_QCHUNK, n_sel * MOBA_BLOCK), s_own], axis=-1)
        p = jax.nn.softmax(s_all, axis=-1).astype(v.dtype)
        p_sel = p[..., :n_sel * MOBA_BLOCK].reshape(B, H, MOBA_QCHUNK, n_sel, MOBA_BLOCK)
        p_own = p[..., n_sel * MOBA_BLOCK:]
        return (jnp.einsum("bhqnk,bhqnkd->bhqd", p_sel, v_sel)
                + jnp.einsum("bhqk,bhkd->bhqd", p_own, v_own))

    outs = lax.map(one_chunk, (jnp.arange(nq), to_chunks(q), to_chunks(sel), to_chunks(sel_valid)))
    return jnp.moveaxis(outs, 0, 2).reshape(B, H, s_pad, dh)[:, :, :S]


def peer_ffn(xn, wq, subkeys, u, v):
    B, S, D = xn.shape
    T = B * S
    xt = xn.reshape(T, D)
    qry = (xt @ wq).reshape(T, PEER_HEADS, 2, PEER_QDIM // 2)
    s = jnp.einsum("thpd,hpnd->thpn", qry, subkeys, preferred_element_type=jnp.float32)
    s1, i1 = lax.top_k(s[:, :, 0], PEER_TOPK)
    s2, i2 = lax.top_k(s[:, :, 1], PEER_TOPK)
    cand = (s1[..., :, None] + s2[..., None, :]).reshape(T, PEER_HEADS, PEER_TOPK * PEER_TOPK)
    cand_id = (i1[..., :, None] * PEER_NKEYS + i2[..., None, :]).reshape(T, PEER_HEADS, PEER_TOPK * PEER_TOPK)
    top_s, top_pos = lax.top_k(cand, PEER_TOPK)
    eid = jnp.take_along_axis(cand_id, top_pos, axis=-1)
    g = jax.nn.softmax(top_s, axis=-1)
    nt = T // PEER_TCHUNK
    kk = PEER_HEADS * PEER_TOPK

    def one(args):
        xc, eidc, gc = args
        u_sel = u[eidc]
        act = jax.nn.gelu(jnp.einsum("td,tkd->tk", xc, u_sel, preferred_element_type=jnp.float32),
                          approximate=False)
        w = (gc * act).astype(xc.dtype)
        return jnp.einsum("tk,tkd->td", w, v[eidc])

    out = lax.map(one, (xt.reshape(nt, PEER_TCHUNK, D), eid.reshape(nt, PEER_TCHUNK, kk),
                        g.reshape(nt, PEER_TCHUNK, kk)))
    return out.reshape(B, S, D)


def setup_inputs(seed: int = 0) -> dict:
    key = jax.random.key(seed)
    ks = jax.random.split(key, 20)
    D = D_MODEL
    nrm = jax.random.normal
    f32 = jnp.float32
    x = nrm(ks[0], (BATCH, SEQ, D), f32)
    c = nrm(ks[1], (BATCH, D), f32)
    w_ada = nrm(ks[2], (DEPTH, D, 6 * D), f32) * (0.5 * D ** -0.5)
    b_ada = 0.01 * nrm(ks[3], (DEPTH, 6 * D), f32)
    norm1_w = 1.0 + 0.02 * nrm(ks[4], (DEPTH, D), f32)
    w_in = nrm(ks[5], (DEPTH, D, IN_COLS), f32) * D ** -0.5
    conv_w = nrm(ks[6], (DEPTH, CONV_WIDTH, 2 * MLSTM_QK_WIDTH), f32) * CONV_WIDTH ** -0.5
    conv_b = 0.01 * nrm(ks[7], (DEPTH, 2 * MLSTM_QK_WIDTH), f32)
    b_igate = 0.1 * nrm(ks[8], (DEPTH, MLSTM_HEADS), f32)
    b_fgate = jnp.linspace(3.0, 6.0, MLSTM_HEADS, dtype=f32)[None, :] + 0.1 * nrm(ks[9], (DEPTH, MLSTM_HEADS), f32)
    mlstm_norm_w = 1.0 + 0.02 * nrm(ks[10], (DEPTH, MLSTM_HEADS, MLSTM_DV), f32)
    w_out = nrm(ks[11], (DEPTH, MIX_WIDTH, D), f32) * MIX_WIDTH ** -0.5
    norm2_w = 1.0 + 0.02 * nrm(ks[12], (DEPTH, D), f32)
    peer_wq = nrm(ks[13], (DEPTH, D, PEER_HEADS * PEER_QDIM), f32) * D ** -0.5
    peer_subkeys = nrm(ks[14], (DEPTH, PEER_HEADS, 2, PEER_NKEYS, PEER_QDIM // 2), f32) * (PEER_QDIM // 2) ** -0.5
    peer_u = nrm(ks[15], (DEPTH, PEER_EXPERTS, D), f32) * D ** -0.5
    peer_v = nrm(ks[16], (DEPTH, PEER_EXPERTS, D), f32) * PEER_HEADS ** -0.5
    final_norm_w = 1.0 + 0.02 * nrm(ks[17], (D,), f32)
    return {"x": x, "c": c, "w_ada": w_ada, "b_ada": b_ada, "norm1_w": norm1_w, "w_in": w_in,
            "conv_w": conv_w, "conv_b": conv_b, "b_igate": b_igate, "b_fgate": b_fgate,
            "mlstm_norm_w": mlstm_norm_w, "w_out": w_out, "norm2_w": norm2_w, "peer_wq": peer_wq,
            "peer_subkeys": peer_subkeys, "peer_u": peer_u, "peer_v": peer_v, "final_norm_w": final_norm_w}


def reference(x, c, w_ada, b_ada, norm1_w, w_in, conv_w, conv_b, b_igate, b_fgate, mlstm_norm_w,
              w_out, norm2_w, peer_wq, peer_subkeys, peer_u, peer_v, final_norm_w):
    B, S, D = x.shape
    positions = jnp.arange(S)
    for l in range(DEPTH):
        mod = jax.nn.silu(c) @ w_ada[l] + b_ada[l]
        shift1, scale1, gate1, shift2, scale2, gate2 = (m[:, None, :] for m in jnp.split(mod, 6, axis=-1))

        xn = rmsnorm(x, norm1_w[l]) * (1 + scale1) + shift1
        p = xn @ w_in[l]

        qk = jax.nn.silu(causal_depthwise_conv(p[..., COL_MQ:COL_MV], conv_w[l], conv_b[l]))
        mq = qk[..., :MLSTM_QK_WIDTH].reshape(B, S, MLSTM_HEADS, MLSTM_DK).transpose(0, 2, 1, 3)
        mk = qk[..., MLSTM_QK_WIDTH:].reshape(B, S, MLSTM_HEADS, MLSTM_DK).transpose(0, 2, 1, 3)
        mv = p[..., COL_MV:COL_MO].reshape(B, S, MLSTM_HEADS, MLSTM_DV).transpose(0, 2, 1, 3)
        o_gate = jax.nn.sigmoid(p[..., COL_MO:COL_MI])
        i_pre = softcap(p[..., COL_MI:COL_MF] + b_igate[l]).transpose(0, 2, 1)
        f_pre = softcap(p[..., COL_MF:COL_AQ] + b_fgate[l]).transpose(0, 2, 1)
        h = mlstm_chunkwise(mq, mk, mv, i_pre, f_pre)
        h = h * lax.rsqrt(jnp.mean(h * h, axis=-1, keepdims=True) + EPS) * mlstm_norm_w[l][:, None, :].astype(jnp.float32)
        h_mlstm = h.astype(x.dtype).transpose(0, 2, 1, 3).reshape(B, S, MLSTM_WIDTH) * o_gate

        aq = p[..., COL_AQ:COL_AK].reshape(B, S, MOBA_HEADS, MOBA_HEAD_DIM).transpose(0, 2, 1, 3)
        ak = p[..., COL_AK:COL_AV].reshape(B, S, MOBA_HEADS, MOBA_HEAD_DIM).transpose(0, 2, 1, 3)
        av = p[..., COL_AV:IN_COLS].reshape(B, S, MOBA_HEADS, MOBA_HEAD_DIM).transpose(0, 2, 1, 3)
        h_moba = moba_attention(rotary(aq, positions), rotary(ak, positions), av)
        h_moba = h_moba.transpose(0, 2, 1, 3).reshape(B, S, MOBA_WIDTH)

        mixed = jnp.concatenate([h_mlstm, h_moba], axis=-1) @ w_out[l]
        x = x + gate1 * mixed

        xn2 = rmsnorm(x, norm2_w[l]) * (1 + scale2) + shift2
        x = x + gate2 * peer_ffn(xn2, peer_wq[l], peer_subkeys[l], peer_u[l], peer_v[l])
    return rmsnorm(x, final_norm_w)
```

```python
import functools
import math

import jax
import jax.numpy as jnp
from jax import lax
from jax.experimental import pallas as pl
from jax.experimental.pallas import tpu as pltpu

F32 = jnp.float32
BF16 = jnp.bfloat16

MLSTM_HEADS = 4
MLSTM_DK = 256
MLSTM_DV = 512
MLSTM_CHUNK = 128
CONV_WIDTH = 4
GATE_SOFTCAP = 15.0
MOBA_HEAD_DIM = 128
MOBA_BLOCK = 256
MOBA_TOPK = 3
ROPE_THETA = 10000.0
PEER_HEADS = 8
PEER_NKEYS = 128
PEER_TOPK = 16
EPS = 1e-6

LANES = 128
SUBLANES = 8
VMEM_LIMIT_BYTES = 56 * 1024 * 1024

NEG = -1e30
LOG2E = 1.4426950408889634


def _params(semantics, vmem=VMEM_LIMIT_BYTES):
    return pltpu.CompilerParams(dimension_semantics=semantics, vmem_limit_bytes=vmem)


def _sigmoid(t):
    return 1.0 / (1.0 + jnp.exp(-t))


def _silu(t):
    return t * _sigmoid(t)


def _ada_kernel(c_ref, w_ref, b_ref, o_ref):
    sc = _silu(c_ref[...]).astype(BF16)
    o_ref[...] = jnp.dot(sc, w_ref[...].astype(BF16), preferred_element_type=F32) + b_ref[...]


def _ada(c, w_ada, b_ada, tn=512):
    bsz, d = c.shape
    n = w_ada.shape[1]
    cp = jnp.zeros((SUBLANES, d), F32).at[:bsz].set(c)
    out = pl.pallas_call(
        _ada_kernel,
        out_shape=jax.ShapeDtypeStruct((SUBLANES, n), F32),
        grid=(n // tn,),
        in_specs=[pl.BlockSpec((SUBLANES, d), lambda j: (0, 0)),
                  pl.BlockSpec((d, tn), lambda j: (0, j)),
                  pl.BlockSpec((1, tn), lambda j: (0, j))],
        out_specs=pl.BlockSpec((SUBLANES, tn), lambda j: (0, j)),
        compiler_params=_params(("arbitrary",)),
        name="ada",
    )(cp, w_ada, b_ada.reshape(1, n))
    return out[:bsz]


def _modnorm(x, nw, sc, sh):
    y = x * lax.rsqrt(jnp.mean(x * x, axis=-1, keepdims=True) + EPS) * nw
    return y * (1.0 + sc) + sh


def _inproj_kernel(x_ref, nw_ref, sc_ref, sh_ref, w_ref, wg_ref, o_ref, g_ref, xn_scr):
    @pl.when(pl.program_id(1) == 0)
    def _():
        xn_scr[...] = _modnorm(x_ref[...], nw_ref[...], sc_ref[...], sh_ref[...]).astype(BF16)
        g_ref[...] = jnp.dot(xn_scr[...], wg_ref[...], preferred_element_type=F32)

    o_ref[...] = jnp.dot(xn_scr[...], w_ref[...], preferred_element_type=F32)


def _inproj(x2, nw, sc, sh, w, wg, seq, tm=512, tn=1024):
    t, d = x2.shape
    n = w.shape[1]
    tpb = seq // tm
    return pl.pallas_call(
        _inproj_kernel,
        out_shape=(jax.ShapeDtypeStruct((t, n), F32), jax.ShapeDtypeStruct((t, LANES), F32)),
        grid=(t // tm, n // tn),
        in_specs=[pl.BlockSpec((tm, d), lambda i, j: (i, 0)),
                  pl.BlockSpec((1, d), lambda i, j: (0, 0)),
                  pl.BlockSpec((None, 1, d), lambda i, j: (i // tpb, 0, 0)),
                  pl.BlockSpec((None, 1, d), lambda i, j: (i // tpb, 0, 0)),
                  pl.BlockSpec((d, tn), lambda i, j: (0, j)),
                  pl.BlockSpec((d, LANES), lambda i, j: (0, 0))],
        out_specs=(pl.BlockSpec((tm, tn), lambda i, j: (i, j)),
                   pl.BlockSpec((tm, LANES), lambda i, j: (i, 0))),
        scratch_shapes=[pltpu.VMEM((tm, d), BF16)],
        compiler_params=_params(("parallel", "arbitrary")),
        name="in_proj",
    )(x2, nw, sc, sh, w, wg)


def _mlstm_kernel(q_ref, k_ref, v_ref, og_ref, g_ref, cw_ref, cb_ref, gb_ref, nw_ref, o_ref,
                  ext_scr, c_scr, n_scr, m_scr):
    L = MLSTM_CHUNK
    H, DK, DV = MLSTM_HEADS, MLSTM_DK, MLSTM_DV
    QK = H * DK

    @pl.when(pl.program_id(1) == 0)
    def _():
        ext_scr[0:SUBLANES, :] = jnp.zeros((SUBLANES, 2 * QK), F32)
        c_scr[...] = jnp.zeros_like(c_scr)
        n_scr[...] = jnp.zeros_like(n_scr)
        m_scr[...] = jnp.zeros_like(m_scr)

    ext_scr[SUBLANES:SUBLANES + L, 0:QK] = q_ref[...]
    ext_scr[SUBLANES:SUBLANES + L, QK:2 * QK] = k_ref[...]
    conv = cb_ref[...] + cw_ref[0:1, :] * ext_scr[pl.ds(SUBLANES - 3, L), :]
    for j in range(1, CONV_WIDTH):
        conv = conv + cw_ref[j:j + 1, :] * ext_scr[pl.ds(SUBLANES - 3 + j, L), :]
    ext_scr[0:SUBLANES, :] = ext_scr[L:L + SUBLANES, :]
    qk = _silu(conv)

    gt = GATE_SOFTCAP * jnp.tanh((g_ref[...] + gb_ref[...]) * (1.0 / GATE_SOFTCAP))
    lane = lax.broadcasted_iota(jnp.int32, (L, LANES), 1)
    row = lax.broadcasted_iota(jnp.int32, (L, LANES), 0)
    logf = jnp.minimum(gt, 0.0) - jnp.log1p(jnp.exp(-jnp.abs(gt)))
    cs = jnp.where((lane >= H) & (lane < 2 * H), logf, 0.0)
    shift = 1
    while shift < L:
        cs = cs + jnp.where(row >= shift, pltpu.roll(cs, shift, 0), 0.0)
        shift *= 2
    gb = jnp.where(lane < H, gt, cs)
    gbt = gb.T

    tri = lax.broadcasted_iota(jnp.int32, (L, L), 1) <= lax.broadcasted_iota(jnp.int32, (L, L), 0)

    for h in range(H):
        qf = qk[:, h * DK:(h + 1) * DK] * (DK ** -0.5)
        kf = qk[:, QK + h * DK:QK + (h + 1) * DK]
        qb = qf.astype(BF16)
        vb = v_ref[:, h * DV:(h + 1) * DV].astype(BF16)
        b_col = gb[:, H + h:H + h + 1]
        li_col = gb[:, h:h + 1]
        b_row = gbt[H + h:H + h + 1, :]
        li_row = gbt[h:h + 1, :]
        g = b_col[L - 1:L, :]
        m_prev = m_scr[h, 0:1, 0:1]

        dmat = jnp.where(tri, b_col - b_row + li_row, -jnp.inf)
        inter = b_col + m_prev
        m_t = jnp.maximum(inter, jnp.max(dmat, axis=1, keepdims=True))
        s = lax.dot_general(qb, kf.astype(BF16), (((1,), (1,)), ((), ())),
                            preferred_element_type=F32) * jnp.exp(dmat - m_t)
        a_inter = jnp.exp(inter - m_t)
        num = (jnp.dot(s.astype(BF16), vb, preferred_element_type=F32)
               + a_inter * jnp.dot(qb, c_scr[h].astype(BF16), preferred_element_type=F32))
        den = (jnp.sum(s, axis=1, keepdims=True)
               + a_inter * jnp.sum(qf * n_scr[h:h + 1, :], axis=1, keepdims=True))
        hh = num / jnp.maximum(jnp.abs(den), jnp.exp(-m_t))

        hn = hh * lax.rsqrt(jnp.mean(hh * hh, axis=1, keepdims=True) + EPS) * nw_ref[:, h * DV:(h + 1) * DV]
        o_ref[:, h * DV:(h + 1) * DV] = (hn * _sigmoid(og_ref[:, h * DV:(h + 1) * DV])).astype(o_ref.dtype)

        wl = g - b_col + li_col
        m_new = jnp.maximum(g + m_prev, jnp.max(wl, axis=0, keepdims=True))
        decay = jnp.exp(g + m_prev - m_new)
        kw = kf * jnp.exp(wl - m_new)
        c_scr[h] = decay * c_scr[h] + jnp.dot(kw.T.astype(BF16), vb, preferred_element_type=F32)
        n_scr[h:h + 1, :] = decay * n_scr[h:h + 1, :] + jnp.sum(kw, axis=0, keepdims=True)
        m_scr[h] = jnp.broadcast_to(m_new, (SUBLANES, LANES))


def _mlstm(p, gates, conv_w, conv_b, gate_bias, norm_w, bsz, seq, col_q, col_k, col_v, col_o):
    L = MLSTM_CHUNK
    H, DK, DV = MLSTM_HEADS, MLSTM_DK, MLSTM_DV
    QK, VW = H * DK, H * DV
    nc = seq // L
    t = bsz * seq
    return pl.pallas_call(
        _mlstm_kernel,
        out_shape=jax.ShapeDtypeStruct((t, VW), BF16),
        grid=(bsz, nc),
        in_specs=[pl.BlockSpec((L, QK), lambda b, c: (b * nc + c, col_q // QK)),
                  pl.BlockSpec((L, QK), lambda b, c: (b * nc + c, col_k // QK)),
                  pl.BlockSpec((L, VW), lambda b, c: (b * nc + c, col_v // VW)),
                  pl.BlockSpec((L, VW), lambda b, c: (b * nc + c, col_o // VW)),
                  pl.BlockSpec((L, LANES), lambda b, c: (b * nc + c, 0)),
                  pl.BlockSpec((CONV_WIDTH, 2 * QK), lambda b, c: (0, 0)),
                  pl.BlockSpec((1, 2 * QK), lambda b, c: (0, 0)),
                  pl.BlockSpec((1, LANES), lambda b, c: (0, 0)),
                  pl.BlockSpec((1, VW), lambda b, c: (0, 0))],
        out_specs=pl.BlockSpec((L, VW), lambda b, c: (b * nc + c, 0)),
        scratch_shapes=[pltpu.VMEM((SUBLANES + L, 2 * QK), F32),
                        pltpu.VMEM((H, DK, DV), F32),
                        pltpu.VMEM((SUBLANES, DK), F32),
                        pltpu.VMEM((H, SUBLANES, LANES), F32)],
        compiler_params=_params(("parallel", "arbitrary")),
        name="mlstm",
    )(p, p, p, p, gates, conv_w, conv_b, gate_bias, norm_w)


def _rope(t, cos, sin_signed):
    return t * cos + pltpu.roll(t, MOBA_HEAD_DIM // 2, 1) * sin_signed


def _moba_kernel(q_ref, k_ref, v_ref, cq_ref, sq_ref, ck_ref, sk_ref, o_ref, kr_scr, vb_scr, km_scr):
    BLK = MOBA_BLOCK
    S = k_ref.shape[0]
    nb = S // BLK
    j = pl.program_id(2)
    scale = MOBA_HEAD_DIM ** -0.5

    @pl.when(j == 0)
    def _():
        kr = _rope(k_ref[...], ck_ref[...], sk_ref[...])
        kr_scr[...] = kr.astype(BF16)
        vb_scr[...] = v_ref[...].astype(BF16)
        km_scr[...] = jnp.zeros_like(km_scr)
        km_scr[0:nb, :] = jnp.mean(kr.reshape(nb, BLK, MOBA_HEAD_DIM), axis=1).astype(BF16)

    qr = _rope(q_ref[...], cq_ref[...], sq_ref[...]).astype(BF16)

    gate = lax.dot_general(qr, km_scr[...], (((1,), (1,)), ((), ())), preferred_element_type=F32)
    lane = lax.broadcasted_iota(jnp.int32, (BLK, LANES), 1)
    past = lane < j
    gm = jnp.where(past, gate, -jnp.inf)
    thr = jnp.max(gm, axis=1, keepdims=True)
    for _ in range(MOBA_TOPK - 1):
        gm = jnp.where(gm >= thr, -jnp.inf, gm)
        thr = jnp.max(gm, axis=1, keepdims=True)
    bias = jnp.where(past & (gate >= thr), 0.0, NEG)

    def scores(n):
        kb = kr_scr[pl.ds(pl.multiple_of(n * BLK, BLK), BLK), :]
        return lax.dot_general(qr, kb, (((1,), (1,)), ((), ())), preferred_element_type=F32) * scale

    def values(n):
        return vb_scr[pl.ds(pl.multiple_of(n * BLK, BLK), BLK), :]

    causal = (lax.broadcasted_iota(jnp.int32, (BLK, BLK), 1) <= lax.broadcasted_iota(jnp.int32, (BLK, BLK), 0))
    s = jnp.where(causal, scores(j), NEG)
    m0 = jnp.max(s, axis=1, keepdims=True)
    p0 = jnp.exp(s - m0)
    l0 = jnp.sum(p0, axis=1, keepdims=True)
    acc0 = jnp.dot(p0.astype(BF16), values(j), preferred_element_type=F32)

    def body(n, carry):
        m, l, acc = carry
        bcol = jnp.sum(jnp.where(lane == n, bias, 0.0), axis=1, keepdims=True)
        sn = scores(n) + bcol
        m_new = jnp.maximum(m, jnp.max(sn, axis=1, keepdims=True))
        alpha = jnp.exp(m - m_new)
        pn = jnp.exp(sn - m_new)
        l = alpha * l + jnp.sum(pn, axis=1, keepdims=True)
        acc = alpha * acc + jnp.dot(pn.astype(BF16), values(n), preferred_element_type=F32)
        return m_new, l, acc

    _, l, acc = lax.fori_loop(0, j, body, (m0, l0, acc0))
    o_ref[...] = (acc / l).astype(o_ref.dtype)


def _rope_tables(seq):
    half = MOBA_HEAD_DIM // 2
    inv = ROPE_THETA ** (-jnp.arange(half, dtype=F32) / half)
    ang = jnp.arange(seq, dtype=F32)[:, None] * inv[None, :]
    cos, sin = jnp.cos(ang), jnp.sin(ang)
    return jnp.concatenate([cos, cos], axis=-1), jnp.concatenate([-sin, sin], axis=-1)


def _moba(p, bsz, seq, heads, col_q, col_k, col_v):
    BLK, DH = MOBA_BLOCK, MOBA_HEAD_DIM
    nq = seq // BLK
    t = bsz * seq
    cos, sin = _rope_tables(seq)
    cq, ck, cv = col_q // DH, col_k // DH, col_v // DH
    return pl.pallas_call(
        _moba_kernel,
        out_shape=jax.ShapeDtypeStruct((t, heads * DH), BF16),
        grid=(bsz, heads, nq),
        in_specs=[pl.BlockSpec((BLK, DH), lambda b, h, j: (b * nq + j, cq + h)),
                  pl.BlockSpec((seq, DH), lambda b, h, j: (b, ck + h)),
                  pl.BlockSpec((seq, DH), lambda b, h, j: (b, cv + h)),
                  pl.BlockSpec((BLK, DH), lambda b, h, j: (j, 0)),
                  pl.BlockSpec((BLK, DH), lambda b, h, j: (j, 0)),
                  pl.BlockSpec((seq, DH), lambda b, h, j: (0, 0)),
                  pl.BlockSpec((seq, DH), lambda b, h, j: (0, 0))],
        out_specs=pl.BlockSpec((BLK, DH), lambda b, h, j: (b * nq + j, h)),
        scratch_shapes=[pltpu.VMEM((seq, DH), BF16),
                        pltpu.VMEM((seq, DH), BF16),
                        pltpu.VMEM((LANES, DH), BF16)],
        compiler_params=_params(("parallel", "parallel", "arbitrary")),
        name="moba",
    )(p, p, p, cos, sin, cos, sin)


def _outproj_kernel(hm_ref, ha_ref, wt_ref, wb_ref, x_ref, g_ref, o_ref):
    acc = jnp.dot(hm_ref[...], wt_ref[...], preferred_element_type=F32)
    acc = acc + jnp.dot(ha_ref[...], wb_ref[...], preferred_element_type=F32)
    o_ref[...] = x_ref[...] + g_ref[...] * acc


def _outproj(hm, ha, w, x2, gate1, seq, tm=512, tn=1024):
    t, d = x2.shape
    km, ka = hm.shape[1], ha.shape[1]
    assert km == ka
    tpb = seq // tm
    return pl.pallas_call(
        _outproj_kernel,
        out_shape=jax.ShapeDtypeStruct((t, d), F32),
        grid=(t // tm, d // tn),
        in_specs=[pl.BlockSpec((tm, km), lambda i, j: (i, 0)),
                  pl.BlockSpec((tm, ka), lambda i, j: (i, 0)),
                  pl.BlockSpec((km, tn), lambda i, j: (0, j)),
                  pl.BlockSpec((ka, tn), lambda i, j: (1, j)),
                  pl.BlockSpec((tm, tn), lambda i, j: (i, j)),
                  pl.BlockSpec((None, 1, tn), lambda i, j: (i // tpb, 0, j))],
        out_specs=pl.BlockSpec((tm, tn), lambda i, j: (i, j)),
        compiler_params=_params(("parallel", "arbitrary")),
        name="out_proj",
    )(hm, ha, w, w, x2, gate1)


def _peerq_kernel(x_ref, nw_ref, sc_ref, sh_ref, w_ref, sk_ref, st_ref, xn_ref):
    @pl.when(pl.program_id(1) == 0)
    def _():
        xn_ref[...] = _modnorm(x_ref[...], nw_ref[...], sc_ref[...], sh_ref[...]).astype(BF16)

    q = jnp.dot(xn_ref[...], w_ref[...], preferred_element_type=F32).astype(BF16)
    half = PEER_NKEYS
    for p in range(2):
        st_ref[p * half:(p + 1) * half, :] = lax.dot_general(
            sk_ref[p], q[:, p * half:(p + 1) * half], (((1,), (1,)), ((), ())), preferred_element_type=F32)


def _peerq(x1, nw, sc, sh, wq, subk, seq, tm=512):
    t, d = x1.shape
    nqc = wq.shape[1]
    tn = 2 * PEER_NKEYS
    tpb = seq // tm
    return pl.pallas_call(
        _peerq_kernel,
        out_shape=(jax.ShapeDtypeStruct((nqc, t), F32), jax.ShapeDtypeStruct((t, d), BF16)),
        grid=(t // tm, nqc // tn),
        in_specs=[pl.BlockSpec((tm, d), lambda i, j: (i, 0)),
                  pl.BlockSpec((1, d), lambda i, j: (0, 0)),
                  pl.BlockSpec((None, 1, d), lambda i, j: (i // tpb, 0, 0)),
                  pl.BlockSpec((None, 1, d), lambda i, j: (i // tpb, 0, 0)),
                  pl.BlockSpec((d, tn), lambda i, j: (0, j)),
                  pl.BlockSpec((2, PEER_NKEYS, PEER_NKEYS), lambda i, j: (j, 0, 0))],
        out_specs=(pl.BlockSpec((tn, tm), lambda i, j: (j, i)),
                   pl.BlockSpec((tm, d), lambda i, j: (i, 0))),
        compiler_params=_params(("parallel", "arbitrary")),
        name="peer_q",
    )(x1, nw, sc, sh, wq, subk)


def _top_values(cur, count):
    vals = []
    for _ in range(count):
        mx = jnp.max(cur, axis=0, keepdims=True)
        vals.append(mx)
        cur = jnp.where(cur >= mx, -jnp.inf, cur)
    return vals


def _router_kernel(st_ref, a_ref, b_ref, lc_ref):
    NK, K = PEER_NKEYS, PEER_TOPK
    lcs = []
    for h in range(PEER_HEADS):
        s1 = st_ref[(2 * h) * NK:(2 * h + 1) * NK, :] * LOG2E
        s2 = st_ref[(2 * h + 1) * NK:(2 * h + 2) * NK, :] * LOG2E
        v1 = _top_values(s1, K)
        v2 = _top_values(s2, K)
        v2_all = jnp.concatenate(v2, axis=0)
        v2_top = v2_all[0:SUBLANES, :]
        cand = jnp.concatenate([v1[0] + v2_all] + [v1[i] + v2_top for i in range(1, K)], axis=0)
        w = _top_values(cand, K + 1)
        z = jnp.ones_like(w[0])
        for r in range(1, K):
            z = z + jnp.exp2(w[r] - w[0])
        shift = w[0] + jnp.log2(z)
        a_ref[h] = jnp.where(s1 >= v1[K - 1], s1 - shift, NEG)
        b_ref[h] = jnp.where(s2 >= v2[K - 1], s2, NEG)
        lcs.append(0.5 * (w[K - 1] + w[K]) - shift)
    lc_ref[...] = jnp.concatenate(lcs, axis=0)


def _router(st, tt=128):
    rows, t = st.shape
    NK, PH = PEER_NKEYS, PEER_HEADS
    return pl.pallas_call(
        _router_kernel,
        out_shape=(jax.ShapeDtypeStruct((PH, NK, t), F32),
                   jax.ShapeDtypeStruct((PH, NK, t), F32),
                   jax.ShapeDtypeStruct((PH, t), F32)),
        grid=(t // tt,),
        in_specs=[pl.BlockSpec((rows, tt), lambda i: (0, i))],
        out_specs=(pl.BlockSpec((PH, NK, tt), lambda i: (0, 0, i)),
                   pl.BlockSpec((PH, NK, tt), lambda i: (0, 0, i)),
                   pl.BlockSpec((PH, tt), lambda i: (0, i))),
        compiler_params=_params(("parallel",)),
        name="peer_router",
    )(st)


def _gelu(t):
    return 0.5 * t * (1.0 + lax.erf(t * (2.0 ** -0.5)))


def _peer_kernel(xn_ref, u_ref, v_ref, a_ref, b_ref, lc_ref, o_ref, g_scr, *, tc):
    e = pl.program_id(1)
    tm = xn_ref.shape[0]
    na = a_ref.shape[0]
    NK = PEER_NKEYS

    @pl.when(e == 0)
    def _():
        o_ref[...] = jnp.zeros_like(o_ref)

    act = lax.dot_general(xn_ref[...], u_ref[...], (((1,), (1,)), ((), ())), preferred_element_type=F32)

    def chunk(c, _):
        t0 = pl.multiple_of(c * tc, tc)
        for al in range(na):
            acc = jnp.zeros((NK, tc), F32)
            for h in range(PEER_HEADS):
                z = a_ref[al, h:h + 1, pl.ds(t0, tc)] + b_ref[h, :, pl.ds(t0, tc)]
                acc = acc + jnp.where(z >= lc_ref[h:h + 1, pl.ds(t0, tc)], jnp.exp2(z), 0.0)
            g_scr[pl.ds(t0, tc), al * NK:(al + 1) * NK] = acc.T
        return 0

    lax.fori_loop(0, tm // tc, chunk, 0)
    w = (g_scr[...] * _gelu(act)).astype(BF16)
    o_ref[...] += jnp.dot(w, v_ref[...], preferred_element_type=F32)


def _peer(xn2, u, v, a4, bt, lct, tm=1024, te=256, tc=256):
    t, d = xn2.shape
    ne = u.shape[0]
    NK, PH = PEER_NKEYS, PEER_HEADS
    na = te // NK
    one = pl.Buffered(1)
    return pl.pallas_call(
        functools.partial(_peer_kernel, tc=tc),
        out_shape=jax.ShapeDtypeStruct((t, d), F32),
        grid=(t // tm, ne // te),
        in_specs=[pl.BlockSpec((tm, d), lambda i, e: (i, 0), pipeline_mode=one),
                  pl.BlockSpec((te, d), lambda i, e: (e, 0)),
                  pl.BlockSpec((te, d), lambda i, e: (e, 0)),
                  pl.BlockSpec((na, PH, tm), lambda i, e: (e, 0, i)),
                  pl.BlockSpec((PH, NK, tm), lambda i, e: (0, 0, i), pipeline_mode=one),
                  pl.BlockSpec((PH, tm), lambda i, e: (0, i), pipeline_mode=one)],
        out_specs=pl.BlockSpec((tm, d), lambda i, e: (i, 0), pipeline_mode=one),
        scratch_shapes=[pltpu.VMEM((tm, te), F32)],
        compiler_params=_params(("parallel", "arbitrary")),
        name="peer_experts",
    )(xn2, u, v, a4, bt, lct)


def _final_kernel(x_ref, pe_ref, g_ref, w_ref, o_ref):
    x = x_ref[...] + g_ref[...] * pe_ref[...]
    o_ref[...] = x * lax.rsqrt(jnp.mean(x * x, axis=-1, keepdims=True) + EPS) * w_ref[...]


def _final(x1, pe, gate2, w, seq, tm=256):
    t, d = x1.shape
    tpb = seq // tm
    return pl.pallas_call(
        _final_kernel,
        out_shape=jax.ShapeDtypeStruct((t, d), F32),
        grid=(t // tm,),
        in_specs=[pl.BlockSpec((tm, d), lambda i: (i, 0)),
                  pl.BlockSpec((tm, d), lambda i: (i, 0)),
                  pl.BlockSpec((None, 1, d), lambda i: (i // tpb, 0, 0)),
                  pl.BlockSpec((1, d), lambda i: (0, 0))],
        out_specs=pl.BlockSpec((tm, d), lambda i: (i, 0)),
        compiler_params=_params(("parallel",)),
        name="final_norm",
    )(x1, pe, gate2, w)


def _layer(x2, c, w_ada, b_ada, norm1_w, w_in, conv_w, conv_b, b_igate, b_fgate, mlstm_norm_w, w_out,
           norm2_w, peer_wq, peer_subkeys, peer_u, peer_v, bsz, seq):
    t, d = x2.shape
    H, DK, DV = MLSTM_HEADS, MLSTM_DK, MLSTM_DV
    qkw, vw = H * DK, H * DV
    mix = w_out.shape[0]
    moba_w = mix - vw
    moba_heads = moba_w // MOBA_HEAD_DIM

    mod = _ada(c, w_ada, b_ada)
    shift1, scale1, gate1, shift2, scale2, gate2 = (m.reshape(bsz, 1, d) for m in jnp.split(mod, 6, axis=-1))

    col_mi = 2 * qkw + 2 * vw
    w_wide = jnp.concatenate([w_in[:, :col_mi], w_in[:, col_mi + 2 * H:]], axis=1).astype(BF16)
    w_gate = jnp.zeros((d, LANES), F32).at[:, :2 * H].set(w_in[:, col_mi:col_mi + 2 * H]).astype(BF16)
    p, gates = _inproj(x2, norm1_w.reshape(1, d), scale1, shift1, w_wide, w_gate, seq)

    gate_bias = jnp.zeros((1, LANES), F32).at[0, :H].set(b_igate).at[0, H:2 * H].set(b_fgate)
    hm = _mlstm(p, gates, conv_w, conv_b.reshape(1, -1), gate_bias, mlstm_norm_w.reshape(1, vw), bsz, seq,
                col_q=0, col_k=qkw, col_v=2 * qkw, col_o=2 * qkw + vw)
    ha = _moba(p, bsz, seq, moba_heads, col_q=col_mi, col_k=col_mi + moba_w, col_v=col_mi + 2 * moba_w)

    x1 = _outproj(hm, ha, w_out.astype(BF16), x2, gate1, seq)

    subk = peer_subkeys.reshape(2 * PEER_HEADS, PEER_NKEYS, -1).astype(BF16)
    st, xn2 = _peerq(x1, norm2_w.reshape(1, d), scale2, shift2, peer_wq.astype(BF16), subk, seq)
    at, bt, lct = _router(st)
    a4 = jnp.transpose(at, (1, 0, 2))
    pe = _peer(xn2, peer_u.astype(BF16), peer_v.astype(BF16), a4, bt, lct)
    return x1, pe, gate2


def kernel(x, c, w_ada, b_ada, norm1_w, w_in, conv_w, conv_b, b_igate, b_fgate, mlstm_norm_w, w_out, norm2_w,
           peer_wq, peer_subkeys, peer_u, peer_v, final_norm_w):
    bsz, seq, d = x.shape
    depth = w_ada.shape[0]
    assert depth == 1
    x1, pe, gate2 = _layer(x.reshape(bsz * seq, d), c, w_ada[0], b_ada[0], norm1_w[0], w_in[0], conv_w[0],
                           conv_b[0], b_igate[0], b_fgate[0], mlstm_norm_w[0], w_out[0], norm2_w[0], peer_wq[0],
                           peer_subkeys[0], peer_u[0], peer_v[0], bsz, seq)
    y = _final(x1, pe, gate2, final_norm_w.reshape(1, d), seq)
    return y.reshape(bsz, seq, d)
```
